```python
import jax, jax.numpy as jnp
from jax import lax
import numpy as np

D_MODEL = 1024
BATCH = 8
SEQ = 2048
DEPTH = 1

D_MIX = D_MODEL
D_CONV = D_MIX // 2
D_ATTN = D_MIX - D_CONV
N_HEADS = 8
HEAD_DIM = D_ATTN // N_HEADS
CONV_K = 31
DILATED_PATTERNS = ((128, 1), (512, 4), (2048, 16))
ATTN_BLOCK = 128
D_FF = 128 * ((8 * D_MODEL // 3 + 127) // 128)
FFN_CONV_K = 3
D_IN = 2 * D_CONV + 3 * D_ATTN
N_MOD = 6
EPS = 1e-6

kernel_name = "hymba_conformer_dilated_attn_convffn_adaln"


def rmsnorm(x, g):
    xf = x.astype(jnp.float32)
    y = xf * lax.rsqrt(jnp.mean(xf * xf, axis=-1, keepdims=True) + EPS)
    return (y * g.astype(jnp.float32)).astype(x.dtype)


def layernorm(x, g, b):
    xf = x.astype(jnp.float32)
    mu = jnp.mean(xf, axis=-1, keepdims=True)
    var = jnp.mean(jnp.square(xf - mu), axis=-1, keepdims=True)
    y = (xf - mu) * lax.rsqrt(var + EPS)
    return (y * g.astype(jnp.float32) + b.astype(jnp.float32)).astype(x.dtype)


def modulate(h, shift, scale):
    return h * (1 + scale[:, None, :]) + shift[:, None, :]


def causal_dwconv(x, w, b):
    k, ch = w.shape
    y = lax.conv_general_dilated(
        x, w[:, None, :].astype(x.dtype), window_strides=(1,), padding=((k - 1, 0),),
        dimension_numbers=("NWC", "WIO", "NWC"), feature_group_count=ch)
    return y + b.astype(x.dtype)


def dilated_window_attn(q, k, v, window, dilation):
    b, s, h, hd = q.shape
    r = dilation
    nw = window // r
    blk = ATTN_BLOCK
    l_sub = -(-s // r)
    lp = -(-l_sub // blk) * blk
    sp = lp * r
    nb = lp // blk

    def to_sub(t):
        t = jnp.pad(t, ((0, 0), (0, sp - s), (0, 0), (0, 0))).reshape(b, lp, r, h, hd)
        return t.transpose(0, 2, 3, 1, 4)

    def kv_blocks(t):
        t = jnp.pad(t, ((0, 0), (0, 0), (0, 0), (blk, 0), (0, 0))).reshape(b, r, h, nb + 1, blk, hd)
        return jnp.concatenate([t[:, :, :, :-1], t[:, :, :, 1:]], axis=4)

    qb = to_sub(q).reshape(b, r, h, nb, blk, hd).astype(jnp.float32)
    kb = kv_blocks(to_sub(k)).astype(jnp.float32)
    vb = kv_blocks(to_sub(v)).astype(jnp.float32)

    scores = jnp.einsum("brhnqd,brhnkd->brhnqk", qb, kb) * (hd ** -0.5)
    qi = jnp.arange(blk)[:, None]
    ki = jnp.arange(2 * blk)[None, :]
    dist = qi + blk - ki
    n_idx = jnp.arange(nb)[:, None, None]
    valid = (dist >= 0) & (dist <= nw) & ((n_idx > 0) | (ki >= blk))
    scores = jnp.where(valid, scores, -jnp.inf)
    m = jnp.max(scores, axis=-1, keepdims=True)
    p = jnp.exp(scores - m)
    denom = jnp.sum(p, axis=-1, keepdims=True)
    o = jnp.einsum("brhnqk,brhnkd->brhnqd", p, vb) / denom
    lse = (m + jnp.log(denom))[..., 0]

    o = o.reshape(b, r, h, lp, hd).transpose(0, 3, 1, 2, 4).reshape(b, sp, h, hd)[:, :s]
    lse = lse.reshape(b, r, h, lp).transpose(0, 3, 1, 2).reshape(b, sp, h)[:, :s]
    return o, lse


def longnet_mixture(q, k, v):
    outs, lses = [], []
    for window, dilation in DILATED_PATTERNS:
        o, lse = dilated_window_attn(q, k, v, window, dilation)
        outs.append(o)
        lses.append(lse)
    alpha = jax.nn.softmax(jnp.stack(lses, axis=0), axis=0)
    o = jnp.sum(alpha[..., None] * jnp.stack(outs, axis=0), axis=0)
    return o.astype(q.dtype)


def setup_inputs(seed: int = 0) -> dict:
    key = jax.random.key(seed)
    ks = jax.random.split(key, 24)
    f32 = jnp.float32
    nrm = lambda kk, shape, scale: jax.random.normal(kk, shape, f32) * scale
    L = DEPTH
    return {
        "x": nrm(ks[0], (BATCH, SEQ, D_MODEL), 1.0),
        "c": nrm(ks[1], (BATCH, D_MODEL), 1.0),
        "w_ada": nrm(ks[2], (L, D_MODEL, N_MOD * D_MODEL), 0.5 * D_MODEL ** -0.5),
        "b_ada": nrm(ks[3], (L, N_MOD * D_MODEL), 0.02),
        "g_norm_mix": 1.0 + nrm(ks[4], (L, D_MODEL), 0.02),
        "w_in": nrm(ks[5], (L, D_MODEL, D_IN), D_MODEL ** -0.5),
        "w_conv_dw": nrm(ks[6], (L, CONV_K, D_CONV), CONV_K ** -0.5),
        "b_conv_dw": nrm(ks[7], (L, D_CONV), 0.02),
        "ln_conv_g": 1.0 + nrm(ks[8], (L, D_CONV), 0.02),
        "ln_conv_b": nrm(ks[9], (L, D_CONV), 0.02),
        "g_conv_out": 1.0 + nrm(ks[10], (L, D_CONV), 0.02),
        "g_attn_out": 1.0 + nrm(ks[11], (L, D_ATTN), 0.02),
        "w_out": nrm(ks[12], (L, D_MIX, D_MODEL), D_MIX ** -0.5),
        "g_norm_ffn": 1.0 + nrm(ks[13], (L, D_MODEL), 0.02),
        "w_up": nrm(ks[14], (L, D_MODEL, 2 * D_FF), D_MODEL ** -0.5),
        "w_ffn_dw": nrm(ks[15], (L, FFN_CONV_K, 2 * D_FF), FFN_CONV_K ** -0.5),
        "b_ffn_dw": nrm(ks[16], (L, 2 * D_FF), 0.02),
        "w_down": nrm(ks[17], (L, D_FF, D_MODEL), D_FF ** -0.5),
        "g_final": 1.0 + nrm(ks[18], (D_MODEL,), 0.02),
    }


def reference(x, c, w_ada, b_ada, g_norm_mix, w_in, w_conv_dw, b_conv_dw, ln_conv_g,
              ln_conv_b, g_conv_out, g_attn_out, w_out, g_norm_ffn, w_up, w_ffn_dw,
              b_ffn_dw, w_down, g_final):
    b, s, d = x.shape
    for i in range(DEPTH):
        mod = jax.nn.silu(c) @ w_ada[i] + b_ada[i]
        sh_m, sc_m, ga_m, sh_f, sc_f, ga_f = jnp.split(mod, N_MOD, axis=-1)

        h = modulate(rmsnorm(x, g_norm_mix[i]), sh_m, sc_m)
        proj = h @ w_in[i]
        a_val, a_gate, q, k, v = jnp.split(
            proj, np.cumsum([D_CONV, D_CONV, D_ATTN, D_ATTN]).tolist(), axis=-1)

        u = a_val * jax.nn.sigmoid(a_gate)
        u = causal_dwconv(u, w_conv_dw[i], b_conv_dw[i])
        u = jax.nn.silu(layernorm(u, ln_conv_g[i], ln_conv_b[i]))

        qh = q.reshape(b, s, N_HEADS, HEAD_DIM)
        kh = k.reshape(b, s, N_HEADS, HEAD_DIM)
        vh = v.reshape(b, s, N_HEADS, HEAD_DIM)
        att = longnet_mixture(qh, kh, vh).reshape(b, s, D_ATTN)

        mixed = jnp.concatenate(
            [rmsnorm(u, g_conv_out[i]), rmsnorm(att, g_attn_out[i])], axis=-1)
        x = x + ga_m[:, None, :] * (mixed @ w_out[i])

        h = modulate(rmsnorm(x, g_norm_ffn[i]), sh_f, sc_f)
        up = causal_dwconv(h @ w_up[i], w_ffn_dw[i], b_ffn_dw[i])
        gate, val = jnp.split(up, 2, axis=-1)
        x = x + ga_f[:, None, :] * ((jax.nn.silu(gate) * val) @ w_down[i])

    return rmsnorm(x, g_final)
```

```python
import functools

import jax
import jax.numpy as jnp
from jax import lax
from jax.experimental import pallas as pl
from jax.experimental.pallas import tpu as pltpu

F32 = jnp.float32
BF16 = jnp.bfloat16

EPS = 1e-6
N_HEADS = 8
N_MOD = 6
CONV_K = 31
FFN_CONV_K = 3
DILATED_PATTERNS = ((128, 1), (512, 4), (2048, 16))
ATTN_BLOCK = 128

V7X_LANES = 128
V7X_SUBLANES = 8
V7X_MXU_DIM = 256
V7X_VMEM_BYTES = 64 * 1024 * 1024

VMEM_LIMIT = 56 * 1024 * 1024

ROW_TILE = 512
CONV_HALO = 32
CONV_ROWS = 32
FFN_CHUNK = V7X_MXU_DIM


def _const_spec(shape):
    nd = len(shape)
    return pl.BlockSpec(shape, lambda *_: (0,) * nd, pipeline_mode=pl.Buffered(1))


def _ada_mod_kernel(c_ref, w_ref, b_ref, o_ref):
    a = jax.nn.silu(c_ref[...]).astype(BF16)
    o_ref[...] = jnp.dot(a, w_ref[...].astype(BF16), preferred_element_type=F32) + b_ref[...]


def _ada_mod(c, w_ada, b_ada):
    b, d = c.shape
    n = w_ada.shape[1]
    nt = d
    return pl.pallas_call(
        _ada_mod_kernel,
        grid=(n // nt,),
        in_specs=[
            pl.BlockSpec((b, d), lambda j: (0, 0)),
            pl.BlockSpec((d, nt), lambda j: (0, j)),
            pl.BlockSpec((1, nt), lambda j: (0, j)),
        ],
        out_specs=pl.BlockSpec((b, nt), lambda j: (0, j)),
        out_shape=jax.ShapeDtypeStruct((b, n), F32),
        compiler_params=pltpu.CompilerParams(dimension_semantics=("arbitrary",), vmem_limit_bytes=VMEM_LIMIT),
        name="ada_mod",
    )(c, w_ada, b_ada.reshape(1, n))


def _mix_in_kernel(x_ref, mod_ref, g_ref, win_ref, wdw_ref, bdw_ref, lng_ref, lnb_ref, gco_ref,
                   ua_ref, q_ref, k_ref, v_ref, ubuf_ref, shift_ref, *, d_conv, d_attn, head_dim):
    tm = x_ref.shape[1]
    x = x_ref[0]
    y = x * lax.rsqrt(jnp.mean(x * x, axis=-1, keepdims=True) + EPS) * g_ref[...]
    h = y * (1.0 + mod_ref[0, 1:2, :]) + mod_ref[0, 0:1, :]
    proj = jnp.dot(h.astype(BF16), win_ref[...], preferred_element_type=F32)

    o = 2 * d_conv
    q_ref[0] = (proj[:, o:o + d_attn] * (head_dim ** -0.5)).astype(BF16)
    k_ref[0] = proj[:, o + d_attn:o + 2 * d_attn].astype(BF16)
    v_ref[0] = proj[:, o + 2 * d_attn:o + 3 * d_attn].astype(BF16)

    @pl.when(pl.program_id(1) == 0)
    def _():
        ubuf_ref[0:CONV_HALO, :] = jnp.zeros((CONV_HALO, d_conv), F32)

    ubuf_ref[CONV_HALO:CONV_HALO + tm, :] = proj[:, :d_conv] * jax.nn.sigmoid(proj[:, d_conv:2 * d_conv])

    first_tap = CONV_HALO - (CONV_K - 1)
    sub = V7X_SUBLANES
    for s in range(1, sub):
        shift_ref[s - 1] = ubuf_ref[pl.ds(s, shift_ref.shape[1]), :]

    def conv_chunk(i, carry):
        r0 = pl.multiple_of(i * CONV_ROWS, CONV_ROWS)
        acc = jnp.broadcast_to(bdw_ref[...], (CONV_ROWS, d_conv))
        for t in range(CONV_K):
            a, s = divmod(first_tap + t, sub)
            rows = pl.ds(pl.multiple_of(r0 + a * sub, sub), CONV_ROWS)
            tap = ubuf_ref[rows, :] if s == 0 else shift_ref[s - 1, rows, :]
            acc = acc + wdw_ref[t:t + 1, :] * tap
        mu = jnp.mean(acc, axis=-1, keepdims=True)
        cen = acc - mu
        var = jnp.mean(cen * cen, axis=-1, keepdims=True)
        z = cen * lax.rsqrt(var + EPS) * lng_ref[...] + lnb_ref[...]
        z = z * jax.nn.sigmoid(z)
        z = z * lax.rsqrt(jnp.mean(z * z, axis=-1, keepdims=True) + EPS) * gco_ref[...]
        ua_ref[0, pl.ds(r0, CONV_ROWS), :] = z.astype(BF16)
        return carry

    lax.fori_loop(0, tm // CONV_ROWS, conv_chunk, 0)
    ubuf_ref[0:CONV_HALO, :] = ubuf_ref[tm:tm + CONV_HALO, :]


def _mix_in(x, mod, g_norm, w_in, w_dw, b_dw, ln_g, ln_b, g_co, *, d_conv, d_attn):
    b, s, d = x.shape
    tm = ROW_TILE
    d_in = w_in.shape[1]
    row = lambda c: pl.BlockSpec((1, tm, c), lambda i, j: (i, j, 0))
    vec = lambda a: a.reshape(1, -1)
    out_sd = jax.ShapeDtypeStruct((b, s, d_conv), BF16)
    return pl.pallas_call(
        functools.partial(_mix_in_kernel, d_conv=d_conv, d_attn=d_attn, head_dim=d_attn // N_HEADS),
        grid=(b, s // tm),
        in_specs=[
            row(d),
            pl.BlockSpec((1, N_MOD, d), lambda i, j: (i, 0, 0)),
            _const_spec((1, d)),
            _const_spec((d, d_in)),
            _const_spec((CONV_K, d_conv)),
            _const_spec((1, d_conv)),
            _const_spec((1, d_conv)),
            _const_spec((1, d_conv)),
            _const_spec((1, d_conv)),
        ],
        out_specs=[row(d_conv), row(d_attn), row(d_attn), row(d_attn)],
        out_shape=[out_sd, jax.ShapeDtypeStruct((b, s, d_attn), BF16),
                   jax.ShapeDtypeStruct((b, s, d_attn), BF16), jax.ShapeDtypeStruct((b, s, d_attn), BF16)],
        scratch_shapes=[pltpu.VMEM((tm + CONV_HALO, d_conv), F32),
                        pltpu.VMEM((V7X_SUBLANES - 1, tm + CONV_HALO - V7X_SUBLANES, d_conv), F32)],
        compiler_params=pltpu.CompilerParams(dimension_semantics=("arbitrary", "arbitrary"),
                                             vmem_limit_bytes=VMEM_LIMIT),
        name="mix_in",
    )(x, mod, vec(g_norm), w_in, w_dw, vec(b_dw), vec(ln_g), vec(ln_b), vec(g_co))


def _attn_kernel(q_ref, k_ref, v_ref, o_ref, lse_ref, *, n_sub, n_blk, d_attn):
    blk = ATTN_BLOCK
    pair_w = V7X_LANES
    head_dim = d_attn // N_HEADS
    qi = lax.broadcasted_iota(jnp.int32, (blk, blk), 0)
    ki = lax.broadcasted_iota(jnp.int32, (blk, blk), 1)
    lane = lax.broadcasted_iota(jnp.int32, (blk, pair_w), 1)
    nt = (((1,), (1,)), ((), ()))

    def one_block(n, c0, with_prev):
        aligned = (lambda a: a) if isinstance(n, int) else (lambda a: pl.multiple_of(a, blk))
        rows = pl.ds(aligned(n * blk), blk)
        cols = slice(c0, c0 + pair_w)
        q2 = q_ref[0, rows, cols]
        k_cur = k_ref[0, rows, cols]
        v_cur = v_ref[0, rows, cols]
        if with_prev:
            prow = pl.ds(aligned((n - 1) * blk), blk)
            k_prev = k_ref[0, prow, cols]
            v_prev = v_ref[0, prow, cols]
        outs, lses = [], []
        for hh in range(pair_w // head_dim):
            in_head = (lane >= hh * head_dim) & (lane < (hh + 1) * head_dim)
            qh = jnp.where(in_head, q2, jnp.zeros_like(q2))
            s_cur = jnp.where(ki <= qi, lax.dot_general(qh, k_cur, nt, preferred_element_type=F32), -jnp.inf)
            m = jnp.max(s_cur, axis=-1, keepdims=True)
            if with_prev:
                s_prev = jnp.where(ki >= qi, lax.dot_general(qh, k_prev, nt, preferred_element_type=F32), -jnp.inf)
                m = jnp.maximum(m, jnp.max(s_prev, axis=-1, keepdims=True))
            p_cur = jnp.exp(s_cur - m)
            den = jnp.sum(p_cur, axis=-1, keepdims=True)
            acc = jnp.dot(p_cur.astype(BF16), v_cur, preferred_element_type=F32)
            if with_prev:
                p_prev = jnp.exp(s_prev - m)
                den = den + jnp.sum(p_prev, axis=-1, keepdims=True)
                acc = acc + jnp.dot(p_prev.astype(BF16), v_prev, preferred_element_type=F32)
            outs.append((in_head, acc / den))
            lses.append((in_head, m + jnp.log(den)))
        o2 = jnp.zeros((blk, pair_w), F32)
        l2 = jnp.zeros((blk, pair_w), F32)
        for (msk, o_h), (_, l_h) in zip(outs, lses):
            o2 = jnp.where(msk, o_h, o2)
            l2 = jnp.where(msk, l_h, l2)
        o_ref[0, rows, cols] = o2.astype(BF16)
        lse_ref[0, rows, cols] = l2

    for g in range(n_sub):
        for pr in range(d_attn // pair_w):
            c0 = g * d_attn + pr * pair_w
            one_block(0, c0, False)
            if n_blk > 1:
                def body(n, carry, c0=c0):
                    one_block(n, c0, True)
                    return carry
                lax.fori_loop(1, n_blk, body, 0)


def _dilated_attn(q, k, v, dilation):
    b, s, d_attn = q.shape
    r = dilation
    l_sub = s // r
    assert s % r == 0 and l_sub % ATTN_BLOCK == 0
    n_sub = min(r, 4)
    view = lambda a: a.reshape(b, l_sub, r * d_attn)
    spec = pl.BlockSpec((1, l_sub, n_sub * d_attn), lambda i, j: (i, 0, j))
    o, lse = pl.pallas_call(
        functools.partial(_attn_kernel, n_sub=n_sub, n_blk=l_sub // ATTN_BLOCK, d_attn=d_attn),
        grid=(b, r // n_sub),
        in_specs=[spec, spec, spec],
        out_specs=[spec, spec],
        out_shape=[jax.ShapeDtypeStruct((b, l_sub, r * d_attn), BF16),
                   jax.ShapeDtypeStruct((b, l_sub, r * d_attn), F32)],
        compiler_params=pltpu.CompilerParams(dimension_semantics=("arbitrary", "arbitrary"),
                                             vmem_limit_bytes=VMEM_LIMIT),
        name=f"dilated_attn_r{r}",
    )(view(q), view(k), view(v))
    return o.reshape(b, s, d_attn), lse.reshape(b, s, d_attn)


def _mix_out_ffn_kernel(x_ref, mod_ref, ua_ref, o1_ref, o2_ref, o3_ref, l1_ref, l2_ref, l3_ref,
                        gat_ref, wout_ref, gffn_ref, wup_ref, wdw_ref, bdw_ref, wdown_ref, gfin_ref,
                        out_ref, cbuf_ref, prev_ref, *, d_conv, d_ff):
    tm = x_ref.shape[1]
    sub = V7X_SUBLANES
    ck = FFN_CHUNK

    l1, l2, l3 = l1_ref[0], l2_ref[0], l3_ref[0]
    m = jnp.maximum(jnp.maximum(l1, l2), l3)
    e1, e2, e3 = jnp.exp(l1 - m), jnp.exp(l2 - m), jnp.exp(l3 - m)
    att = (e1 * o1_ref[0].astype(F32) + e2 * o2_ref[0].astype(F32) + e3 * o3_ref[0].astype(F32)) / (e1 + e2 + e3)
    att = att * lax.rsqrt(jnp.mean(att * att, axis=-1, keepdims=True) + EPS) * gat_ref[...]

    mixed = (jnp.dot(ua_ref[0], wout_ref[0:d_conv, :], preferred_element_type=F32)
             + jnp.dot(att.astype(BF16), wout_ref[d_conv:, :], preferred_element_type=F32))
    x1 = x_ref[0] + mod_ref[0, 2:3, :] * mixed

    y = x1 * lax.rsqrt(jnp.mean(x1 * x1, axis=-1, keepdims=True) + EPS) * gffn_ref[...]
    h = (y * (1.0 + mod_ref[0, 4:5, :]) + mod_ref[0, 3:4, :]).astype(BF16)

    @pl.when(pl.program_id(1) == 0)
    def _():
        prev_ref[...] = jnp.zeros(prev_ref.shape, F32)

    acc = jnp.zeros((tm, x_ref.shape[2]), F32)
    for j in range(d_ff // ck):
        for half in range(2):
            c0 = half * d_ff + j * ck
            dst = slice(half * ck, (half + 1) * ck)
            up = jnp.dot(h, wup_ref[:, c0:c0 + ck], preferred_element_type=F32)
            cbuf_ref[0:sub, dst] = prev_ref[:, c0:c0 + ck]
            cbuf_ref[sub:sub + tm, dst] = up
            prev_ref[:, c0:c0 + ck] = up[tm - sub:, :]
        conv = jnp.broadcast_to(bdw_ref[j:j + 1, :], (tm, 2 * ck))
        for t in range(FFN_CONV_K):
            conv = conv + wdw_ref[j * FFN_CONV_K + t:j * FFN_CONV_K + t + 1, :] * cbuf_ref[pl.ds(sub - (FFN_CONV_K - 1) + t, tm), :]
        gate, val = conv[:, :ck], conv[:, ck:]
        act = (gate * jax.nn.sigmoid(gate) * val).astype(BF16)
        acc = acc + jnp.dot(act, wdown_ref[j * ck:(j + 1) * ck, :], preferred_element_type=F32)

    x2 = x1 + mod_ref[0, 5:6, :] * acc
    out_ref[0] = x2 * lax.rsqrt(jnp.mean(x2 * x2, axis=-1, keepdims=True) + EPS) * gfin_ref[...]


def _mix_out_ffn(x, mod, ua, os_, lses, g_attn, w_out, g_ffn, w_up, w_dw, b_dw, w_down, g_fin, *, d_conv):
    b, s, d = x.shape
    tm = ROW_TILE
    d_attn = os_[0].shape[2]
    d_ff = w_down.shape[0]
    ck = FFN_CHUNK
    nck = d_ff // ck
    assert d_ff % ck == 0
    pair = lambda a: jnp.concatenate([a[..., :d_ff].reshape(a.shape[:-1] + (nck, ck)),
                                      a[..., d_ff:].reshape(a.shape[:-1] + (nck, ck))], axis=-1)
    w_dw_c = pair(w_dw).transpose(1, 0, 2).reshape(nck * FFN_CONV_K, 2 * ck)
    b_dw_c = pair(b_dw)
    row = lambda c: pl.BlockSpec((1, tm, c), lambda i, j: (i, j, 0))
    vec = lambda a: a.reshape(1, -1)
    return pl.pallas_call(
        functools.partial(_mix_out_ffn_kernel, d_conv=d_conv, d_ff=d_ff),
        grid=(b, s // tm),
        in_specs=[
            row(d),
            pl.BlockSpec((1, N_MOD, d), lambda i, j: (i, 0, 0)),
            row(d_conv), row(d_attn), row(d_attn), row(d_attn), row(d_attn), row(d_attn), row(d_attn),
            _const_spec((1, d_attn)),
            _const_spec(w_out.shape),
            _const_spec((1, d)),
            _const_spec(w_up.shape),
            _const_spec(w_dw_c.shape),
            _const_spec(b_dw_c.shape),
            _const_spec(w_down.shape),
            _const_spec((1, d)),
        ],
        out_specs=row(d),
        out_shape=jax.ShapeDtypeStruct((b, s, d), F32),
        scratch_shapes=[pltpu.VMEM((tm + V7X_SUBLANES, 2 * ck), F32), pltpu.VMEM((V7X_SUBLANES, 2 * d_ff), F32)],
        compiler_params=pltpu.CompilerParams(dimension_semantics=("arbitrary", "arbitrary"),
                                             vmem_limit_bytes=VMEM_LIMIT),
        name="mix_out_ffn",
    )(x, mod, ua, *os_, *lses, vec(g_attn), w_out, vec(g_ffn), w_up, w_dw_c, b_dw_c, w_down, vec(g_fin))


def kernel(x, c, w_ada, b_ada, g_norm_mix, w_in, w_conv_dw, b_conv_dw, ln_conv_g, ln_conv_b, g_conv_out,
           g_attn_out, w_out, g_norm_ffn, w_up, w_ffn_dw, b_ffn_dw, w_down, g_final):
    b, s, d = x.shape
    assert w_ada.shape[0] == 1, "the final RMSNorm is fused into the layer kernel: one layer only"
    d_conv = w_conv_dw.shape[2]
    d_attn = g_attn_out.shape[1]
    mod = _ada_mod(c, w_ada[0], b_ada[0]).reshape(b, N_MOD, d)
    ua, q, k, v = _mix_in(x, mod, g_norm_mix[0], w_in[0].astype(BF16), w_conv_dw[0], b_conv_dw[0],
                          ln_conv_g[0], ln_conv_b[0], g_conv_out[0], d_conv=d_conv, d_attn=d_attn)
    os_, lses = zip(*[_dilated_attn(q, k, v, r) for _, r in DILATED_PATTERNS])
    return _mix_out_ffn(x, mod, ua, os_, lses, g_attn_out[0], w_out[0].astype(BF16), g_norm_ffn[0],
                        w_up[0].astype(BF16), w_ffn_dw[0], b_ffn_dw[0], w_down[0].astype(BF16), g_final,
                        d_conv=d_conv)
```

```python
import functools

import jax
import jax.numpy as jnp
from jax import lax
from jax.experimental import pallas as pl
from jax.experimental.pallas import tpu as pltpu

F32 = jnp.float32
BF16 = jnp.bfloat16

EPS = 1e-6
N_HEADS = 8
N_MOD = 6
CONV_K = 31
FFN_CONV_K = 3
DILATED_PATTERNS = ((128, 1), (512, 4), (2048, 16))
ATTN_BLOCK = 128

V7X_LANES = 128
V7X_SUBLANES = 8
V7X_MXU_DIM = 256

VMEM_LIMIT = 56 * 1024 * 1024

ROW_TILE = 512
CONV_HALO = 32
CONV_ROWS = 64
FFN_CHUNK = V7X_MXU_DIM
ATTN_UNROLL = 4


def _const_spec(shape):
    nd = len(shape)
    return pl.BlockSpec(shape, lambda *_: (0,) * nd, pipeline_mode=pl.Buffered(1))


def _ada_mod_kernel(c_ref, w_ref, b_ref, o_ref):
    a = jax.nn.silu(c_ref[...]).astype(BF16)
    o_ref[...] = jnp.dot(a, w_ref[...].astype(BF16), preferred_element_type=F32) + b_ref[...]


def _ada_mod(c, w_ada, b_ada):
    b, d = c.shape
    n = w_ada.shape[1]
    nt = d
    return pl.pallas_call(
        _ada_mod_kernel,
        grid=(n // nt,),
        in_specs=[
            pl.BlockSpec((b, d), lambda j: (0, 0)),
            pl.BlockSpec((d, nt), lambda j: (0, j)),
            pl.BlockSpec((1, nt), lambda j: (0, j)),
        ],
        out_specs=pl.BlockSpec((b, nt), lambda j: (0, j)),
        out_shape=jax.ShapeDtypeStruct((b, n), F32),
        compiler_params=pltpu.CompilerParams(dimension_semantics=("arbitrary",), vmem_limit_bytes=VMEM_LIMIT),
        name="ada_mod",
    )(c, w_ada, b_ada.reshape(1, n))


def _mix_in_kernel(x_ref, mod_ref, g_ref, win_ref, wdw_ref, bdw_ref, lng_ref, lnb_ref, gco_ref,
                   ua_ref, qkv1_ref, qkv4_ref, qkv16_ref, ubuf_ref, shift_ref, slab_ref, conv_ref,
                   *, d_conv, d_attn, head_dim):
    tm = x_ref.shape[1]
    sub = V7X_SUBLANES
    lanes = V7X_LANES
    x = x_ref[0]
    y = x * lax.rsqrt(jnp.mean(x * x, axis=-1, keepdims=True) + EPS) * g_ref[...]
    h = y * (1.0 + mod_ref[0, 1:2, :]) + mod_ref[0, 0:1, :]
    proj = jnp.dot(h.astype(BF16), win_ref[...], preferred_element_type=F32)

    @pl.when(pl.program_id(1) == 0)
    def _():
        ubuf_ref[0:CONV_HALO, :] = jnp.zeros((CONV_HALO, d_conv), F32)

    ubuf_ref[CONV_HALO:CONV_HALO + tm, :] = proj[:, :d_conv] * jax.nn.sigmoid(proj[:, d_conv:2 * d_conv])

    o = 2 * d_conv
    qkv = jnp.concatenate([proj[:, o:o + d_attn] * (head_dim ** -0.5), proj[:, o + d_attn:]], axis=-1)
    qkv1_ref[0, 0] = qkv.astype(BF16)
    n_slab = qkv.shape[1] // lanes
    for i in range(n_slab):
        slab_ref[i] = qkv[:, i * lanes:(i + 1) * lanes]
    for r, ref in ((4, qkv4_ref), (16, qkv16_ref)):
        for rho in range(r):
            for i in range(n_slab):
                ref[0, rho, :, i * lanes:(i + 1) * lanes] = slab_ref[i, pl.ds(rho, tm // r, stride=r), :].astype(BF16)

    first_tap = CONV_HALO - (CONV_K - 1)
    for s in range(1, sub):
        shift_ref[s - 1] = ubuf_ref[pl.ds(s, shift_ref.shape[1]), :]

    def conv_chunk(i, carry):
        r0 = pl.multiple_of(i * CONV_ROWS, CONV_ROWS)
        acc = jnp.broadcast_to(bdw_ref[...], (CONV_ROWS, d_conv))
        for t in range(CONV_K):
            a, s = divmod(first_tap + t, sub)
            rows = pl.ds(pl.multiple_of(r0 + a * sub, sub), CONV_ROWS)
            tap = ubuf_ref[rows, :] if s == 0 else shift_ref[s - 1, rows, :]
            acc = acc + wdw_ref[t:t + 1, :] * tap
        conv_ref[pl.ds(r0, CONV_ROWS), :] = acc
        return carry

    lax.fori_loop(0, tm // CONV_ROWS, conv_chunk, 0)
    ubuf_ref[0:CONV_HALO, :] = ubuf_ref[tm:tm + CONV_HALO, :]

    z = conv_ref[...]
    cen = z - jnp.mean(z, axis=-1, keepdims=True)
    var = jnp.mean(cen * cen, axis=-1, keepdims=True)
    z = cen * lax.rsqrt(var + EPS) * lng_ref[...] + lnb_ref[...]
    z = z * jax.nn.sigmoid(z)
    z = z * lax.rsqrt(jnp.mean(z * z, axis=-1, keepdims=True) + EPS) * gco_ref[...]
    ua_ref[0] = z.astype(BF16)


def _mix_in(x, mod, g_norm, w_in, w_dw, b_dw, ln_g, ln_b, g_co, *, d_conv, d_attn):
    b, s, d = x.shape
    tm = ROW_TILE
    d_in = w_in.shape[1]
    d_qkv = 3 * d_attn
    vec = lambda a: a.reshape(1, -1)
    dils = [r for _, r in DILATED_PATTERNS]
    assert dils == [1, 4, 16] and tm % (16 * 16) == 0
    qkv_spec = lambda r: pl.BlockSpec((1, r, tm // r, d_qkv), lambda i, j: (i, 0, j, 0))
    qkv_shape = lambda r: jax.ShapeDtypeStruct((b, r, s // r, d_qkv), BF16)
    return pl.pallas_call(
        functools.partial(_mix_in_kernel, d_conv=d_conv, d_attn=d_attn, head_dim=d_attn // N_HEADS),
        grid=(b, s // tm),
        in_specs=[
            pl.BlockSpec((1, tm, d), lambda i, j: (i, j, 0)),
            pl.BlockSpec((1, N_MOD, d), lambda i, j: (i, 0, 0)),
            _const_spec((1, d)),
            _const_spec((d, d_in)),
            _const_spec((CONV_K, d_conv)),
            _const_spec((1, d_conv)),
            _const_spec((1, d_conv)),
            _const_spec((1, d_conv)),
            _const_spec((1, d_conv)),
        ],
        out_specs=[pl.BlockSpec((1, tm, d_conv), lambda i, j: (i, j, 0))] + [qkv_spec(r) for r in dils],
        out_shape=[jax.ShapeDtypeStruct((b, s, d_conv), BF16)] + [qkv_shape(r) for r in dils],
        scratch_shapes=[pltpu.VMEM((tm + CONV_HALO, d_conv), F32),
                        pltpu.VMEM((V7X_SUBLANES - 1, tm + CONV_HALO - V7X_SUBLANES, d_conv), F32),
                        pltpu.VMEM((d_qkv // V7X_LANES, tm, V7X_LANES), F32),
                        pltpu.VMEM((tm, d_conv), F32)],
        compiler_params=pltpu.CompilerParams(dimension_semantics=("arbitrary", "arbitrary"),
                                             vmem_limit_bytes=VMEM_LIMIT),
        name="mix_in",
    )(x, mod, vec(g_norm), w_in, w_dw, vec(b_dw), vec(ln_g), vec(ln_b), vec(g_co))


def _attn_kernel(q1_ref, k1_ref, v1_ref, q4_ref, k4_ref, v4_ref, q16_ref, k16_ref, v16_ref, bias_ref,
                 o_ref, acc_ref, m_ref, l_ref, *, head_dim):
    blk = ATTN_BLOCK
    s_len = acc_ref.shape[0]
    lane = lax.broadcasted_iota(jnp.int32, (blk, V7X_LANES), 1)
    head0 = lane < head_dim
    nt = (((1,), (1,)), ((), ()))

    def block_stats(qr, kr, vr, rho, n, with_prev):
        rows = pl.ds(n * blk, blk) if isinstance(n, int) else pl.ds(pl.multiple_of(n * blk, blk), blk)
        if with_prev:
            krows = pl.ds(pl.multiple_of((n - 1) * blk, blk), 2 * blk)
            bias = bias_ref[...]
        else:
            krows = rows
            bias = bias_ref[:, blk:]
        q2 = qr[0, rho, rows, :]
        kk = kr[0, rho, krows, :]
        vv = vr[0, rho, krows, :]
        zero = jnp.zeros_like(q2)
        qs = jnp.concatenate([jnp.where(head0, q2, zero), jnp.where(head0, zero, q2)], axis=0)
        s = lax.dot_general(qs, kk, nt, preferred_element_type=F32) + bias
        m = jnp.max(s, axis=-1, keepdims=True)
        p = jnp.exp(s - m).astype(BF16)
        one = jnp.ones_like(vv)
        vlane = lax.broadcasted_iota(jnp.int32, vv.shape, 1) < head_dim
        out0 = jnp.dot(p[:blk], jnp.where(vlane, vv, one), preferred_element_type=F32)
        out1 = jnp.dot(p[blk:], jnp.where(vlane, one, vv), preferred_element_type=F32)
        o2 = jnp.where(head0, out0, out1)
        l2 = pltpu.roll(jnp.where(head0, out1, out0), head_dim, axis=1)
        m2 = jnp.where(head0, jnp.broadcast_to(m[:blk], (blk, V7X_LANES)), jnp.broadcast_to(m[blk:], (blk, V7X_LANES)))
        return o2, m2, l2

    def merge(idx, o2, m2, l2, final_rows=None):
        m_old = m_ref[idx, :]
        m_new = jnp.maximum(m_old, m2)
        a = jnp.exp(m_old - m_new)
        c = jnp.exp(m2 - m_new)
        acc = a * acc_ref[idx, :] + c * o2
        den = a * l_ref[idx, :] + c * l2
        if final_rows is None:
            acc_ref[idx, :] = acc
            l_ref[idx, :] = den
            m_ref[idx, :] = m_new
        else:
            o_ref[0, final_rows, :] = (acc / den).astype(BF16)

    r = 16
    def body16(g, carry):
        for u in range(ATTN_UNROLL):
            rho = g * ATTN_UNROLL + u
            o2, m2, l2 = block_stats(q16_ref, k16_ref, v16_ref, rho, 0, False)
            idx = pl.ds(rho, blk, stride=16)
            acc_ref[idx, :] = o2
            m_ref[idx, :] = m2
            l_ref[idx, :] = l2
        return carry
    assert s_len // r == blk and r % ATTN_UNROLL == 0
    lax.fori_loop(0, r // ATTN_UNROLL, body16, 0)

    r = 4
    def blocks4(n, with_prev):
        for rho in range(r):
            o2, m2, l2 = block_stats(q4_ref, k4_ref, v4_ref, rho, n, with_prev)
            merge(pl.ds(rho + n * blk * r, blk, stride=r), o2, m2, l2)
    blocks4(0, False)
    def body4(n, carry):
        blocks4(n, True)
        return carry
    lax.fori_loop(1, s_len // r // blk, body4, 0)

    def blocks1(n, with_prev):
        o2, m2, l2 = block_stats(q1_ref, k1_ref, v1_ref, 0, n, with_prev)
        rows = pl.ds(n * blk, blk) if isinstance(n, int) else pl.ds(pl.multiple_of(n * blk, blk), blk)
        merge(rows, o2, m2, l2, final_rows=rows)
    blocks1(0, False)
    n_blk = s_len // blk
    assert (n_blk - 1) % 3 == 0
    def body1(g, carry):
        for u in range(3):
            blocks1(1 + g * 3 + u, True)
        return carry
    lax.fori_loop(0, (n_blk - 1) // 3, body1, 0)


def _dilated_attn(qkv1, qkv4, qkv16, *, d_attn):
    b, _, s, _ = qkv1.shape
    blk = ATTN_BLOCK
    pair_w = V7X_LANES
    n_pair = d_attn // pair_w
    qi = jnp.arange(2 * blk)[:, None] % blk
    ki = jnp.arange(2 * blk)[None, :]
    bias = jnp.where(jnp.where(ki < blk, ki >= qi, ki - blk <= qi), 0.0, -jnp.inf).astype(F32)
    specs = []
    for arr in (qkv1, qkv4, qkv16):
        r, l_sub = arr.shape[1], arr.shape[2]
        for part in range(3):
            specs.append(pl.BlockSpec((1, r, l_sub, pair_w), lambda i, j, part=part: (i, 0, 0, part * n_pair + j)))
    return pl.pallas_call(
        functools.partial(_attn_kernel, head_dim=d_attn // N_HEADS),
        grid=(b, n_pair),
        in_specs=specs + [_const_spec(bias.shape)],
        out_specs=pl.BlockSpec((1, s, pair_w), lambda i, j: (i, 0, j)),
        out_shape=jax.ShapeDtypeStruct((b, s, d_attn), BF16),
        scratch_shapes=[pltpu.VMEM((s, pair_w), F32), pltpu.VMEM((s, pair_w), F32), pltpu.VMEM((s, pair_w), F32)],
        compiler_params=pltpu.CompilerParams(dimension_semantics=("arbitrary", "arbitrary"),
                                             vmem_limit_bytes=VMEM_LIMIT),
        name="dilated_attn",
    )(qkv1, qkv1, qkv1, qkv4, qkv4, qkv4, qkv16, qkv16, qkv16, bias)


def _mix_out_ffn_kernel(x_ref, mod_ref, ua_ref, att_ref, gat_ref, wout_ref, gffn_ref, wup_ref, wdw_ref, bdw_ref,
                        wdown_ref, gfin_ref, out_ref, cbuf_ref, prev_ref, *, d_conv, d_ff):
    tm = x_ref.shape[1]
    sub = V7X_SUBLANES
    ck = FFN_CHUNK

    att = att_ref[0].astype(F32)
    att = att * lax.rsqrt(jnp.mean(att * att, axis=-1, keepdims=True) + EPS) * gat_ref[...]
    mixed = (jnp.dot(ua_ref[0], wout_ref[0:d_conv, :], preferred_element_type=F32)
             + jnp.dot(att.astype(BF16), wout_ref[d_conv:, :], preferred_element_type=F32))
    x1 = x_ref[0] + mod_ref[0, 2:3, :] * mixed

    y = x1 * lax.rsqrt(jnp.mean(x1 * x1, axis=-1, keepdims=True) + EPS) * gffn_ref[...]
    h = (y * (1.0 + mod_ref[0, 4:5, :]) + mod_ref[0, 3:4, :]).astype(BF16)

    @pl.when(pl.program_id(1) == 0)
    def _():
        prev_ref[...] = jnp.zeros(prev_ref.shape, F32)

    acc = jnp.zeros((tm, x_ref.shape[2]), F32)
    for j in range(d_ff // ck):
        for half in range(2):
            c0 = half * d_ff + j * ck
            dst = slice(half * ck, (half + 1) * ck)
            up = jnp.dot(h, wup_ref[:, c0:c0 + ck], preferred_element_type=F32)
            cbuf_ref[0:sub, dst] = prev_ref[:, c0:c0 + ck]
            cbuf_ref[sub:sub + tm, dst] = up
            prev_ref[:, c0:c0 + ck] = up[tm - sub:, :]
        conv = jnp.broadcast_to(bdw_ref[j:j + 1, :], (tm, 2 * ck))
        for t in range(FFN_CONV_K):
            conv = conv + wdw_ref[j * FFN_CONV_K + t:j * FFN_CONV_K + t + 1, :] * cbuf_ref[pl.ds(sub - (FFN_CONV_K - 1) + t, tm), :]
        gate, val = conv[:, :ck], conv[:, ck:]
        act = (gate * jax.nn.sigmoid(gate) * val).astype(BF16)
        acc = acc + jnp.dot(act, wdown_ref[j * ck:(j + 1) * ck, :], preferred_element_type=F32)

    x2 = x1 + mod_ref[0, 5:6, :] * acc
    out_ref[0] = x2 * lax.rsqrt(jnp.mean(x2 * x2, axis=-1, keepdims=True) + EPS) * gfin_ref[...]


def _mix_out_ffn(x, mod, ua, att, g_attn, w_out, g_ffn, w_up, w_dw, b_dw, w_down, g_fin, *, d_conv):
    b, s, d = x.shape
    tm = ROW_TILE
    d_attn = att.shape[2]
    d_ff = w_down.shape[0]
    ck = FFN_CHUNK
    nck = d_ff // ck
    assert d_ff % ck == 0
    pair = lambda a: jnp.concatenate([a[..., :d_ff].reshape(a.shape[:-1] + (nck, ck)),
                                      a[..., d_ff:].reshape(a.shape[:-1] + (nck, ck))], axis=-1)
    w_dw_c = pair(w_dw).transpose(1, 0, 2).reshape(nck * FFN_CONV_K, 2 * ck)
    b_dw_c = pair(b_dw)
    row = lambda c: pl.BlockSpec((1, tm, c), lambda i, j: (i, j, 0))
    vec = lambda a: a.reshape(1, -1)
    return pl.pallas_call(
        functools.partial(_mix_out_ffn_kernel, d_conv=d_conv, d_ff=d_ff),
        grid=(b, s // tm),
        in_specs=[
            row(d),
            pl.BlockSpec((1, N_MOD, d), lambda i, j: (i, 0, 0)),
            row(d_conv), row(d_attn),
            _const_spec((1, d_attn)),
            _const_spec(w_out.shape),
            _const_spec((1, d)),
            _const_spec(w_up.shape),
            _const_spec(w_dw_c.shape),
            _const_spec(b_dw_c.shape),
            _const_spec(w_down.shape),
            _const_spec((1, d)),
        ],
        out_specs=row(d),
        out_shape=jax.ShapeDtypeStruct((b, s, d), F32),
        scratch_shapes=[pltpu.VMEM((tm + V7X_SUBLANES, 2 * ck), F32), pltpu.VMEM((V7X_SUBLANES, 2 * d_ff), F32)],
        compiler_params=pltpu.CompilerParams(dimension_semantics=("arbitrary", "arbitrary"),
                                             vmem_limit_bytes=VMEM_LIMIT),
        name="mix_out_ffn",
    )(x, mod, ua, att, vec(g_attn), w_out, vec(g_ffn), w_up, w_dw_c, b_dw_c, w_down, vec(g_fin))


def kernel(x, c, w_ada, b_ada, g_norm_mix, w_in, w_conv_dw, b_conv_dw, ln_conv_g, ln_conv_b, g_conv_out,
           g_attn_out, w_out, g_norm_ffn, w_up, w_ffn_dw, b_ffn_dw, w_down, g_final):
    b, s, d = x.shape
    assert w_ada.shape[0] == 1, "the final RMSNorm is fused into the layer kernel: one layer only"
    d_conv = w_conv_dw.shape[2]
    d_attn = g_attn_out.shape[1]
    mod = _ada_mod(c, w_ada[0], b_ada[0]).reshape(b, N_MOD, d)
    ua, qkv1, qkv4, qkv16 = _mix_in(x, mod, g_norm_mix[0], w_in[0].astype(BF16), w_conv_dw[0], b_conv_dw[0],
                                    ln_conv_g[0], ln_conv_b[0], g_conv_out[0], d_conv=d_conv, d_attn=d_attn)
    att = _dilated_attn(qkv1, qkv4, qkv16, d_attn=d_attn)
    return _mix_out_ffn(x, mod, ua, att, g_attn_out[0], w_out[0].astype(BF16), g_norm_ffn[0],
                        w_up[0].astype(BF16), w_ffn_dw[0], b_ffn_dw[0], w_down[0].astype(BF16), g_final,
                        d_conv=d_conv)
```

```python
import functools

import jax
import jax.numpy as jnp
from jax import lax
from jax.experimental import pallas as pl
from jax.experimental.pallas import tpu as pltpu

F32 = jnp.float32
BF16 = jnp.bfloat16

EPS = 1e-6
N_HEADS = 8
N_MOD = 6
CONV_K = 31
FFN_CONV_K = 3
DILATED_PATTERNS = ((128, 1), (512, 4), (2048, 16))
ATTN_BLOCK = 128

V7X_LANES = 128
V7X_SUBLANES = 8
V7X_MXU_DIM = 256

VMEM_LIMIT = 56 * 1024 * 1024

ROW_TILE = 512
CONV_HALO = 32
CONV_ROWS = 32
FFN_CHUNK = V7X_MXU_DIM

LOG2_E = 1.4426950408889634


def _qk_scale(head_dim):
    return head_dim ** -0.5 * LOG2_E


def _const_spec(shape):
    nd = len(shape)
    return pl.BlockSpec(shape, lambda *_: (0,) * nd, pipeline_mode=pl.Buffered(1))


def _ada_mod_kernel(c_ref, w_ref, b_ref, o_ref):
    a = jax.nn.silu(c_ref[...]).astype(BF16)
    o_ref[...] = jnp.dot(a, w_ref[...].astype(BF16), preferred_element_type=F32) + b_ref[...]


def _ada_mod(c, w_ada, b_ada):
    b, d = c.shape
    n = w_ada.shape[1]
    nt = d
    return pl.pallas_call(
        _ada_mod_kernel,
        grid=(n // nt,),
        in_specs=[
            pl.BlockSpec((b, d), lambda j: (0, 0)),
            pl.BlockSpec((d, nt), lambda j: (0, j)),
            pl.BlockSpec((1, nt), lambda j: (0, j)),
        ],
        out_specs=pl.BlockSpec((b, nt), lambda j: (0, j)),
        out_shape=jax.ShapeDtypeStruct((b, n), F32),
        compiler_params=pltpu.CompilerParams(dimension_semantics=("arbitrary",), vmem_limit_bytes=VMEM_LIMIT),
        name="ada_mod",
    )(c, w_ada, b_ada.reshape(1, n))


def _mix_in_kernel(x_ref, mod_ref, g_ref, win_ref, wdw_ref, bdw_ref, lng_ref, lnb_ref, gco_ref,
                   ua_ref, qkv1_ref, qkv4_ref, qkv16_ref, ubuf_ref, shift_ref, slab_ref, slab4_ref,
                   *, d_conv, d_attn, q_scale):
    tm = x_ref.shape[1]
    sub = V7X_SUBLANES
    lanes = V7X_LANES
    x = x_ref[0]
    y = x * lax.rsqrt(jnp.mean(x * x, axis=-1, keepdims=True) + EPS) * g_ref[...]
    h = (y * (1.0 + mod_ref[0, 1:2, :]) + mod_ref[0, 0:1, :]).astype(BF16)

    glu = jnp.dot(h, win_ref[:, :2 * d_conv], preferred_element_type=F32)

    @pl.when(pl.program_id(1) == 0)
    def _():
        ubuf_ref[0:CONV_HALO, :] = jnp.zeros((CONV_HALO, d_conv), F32)

    ubuf_ref[CONV_HALO:CONV_HALO + tm, :] = glu[:, :d_conv] * jax.nn.sigmoid(glu[:, d_conv:])

    first_tap = CONV_HALO - (CONV_K - 1)
    for s in range(1, sub):
        shift_ref[s - 1] = ubuf_ref[pl.ds(s, shift_ref.shape[1]), :]

    for r0 in range(0, tm, CONV_ROWS):
        acc = jnp.broadcast_to(bdw_ref[...], (CONV_ROWS, d_conv))
        for t in range(CONV_K):
            a, s = divmod(first_tap + t, sub)
            rows = pl.ds(r0 + a * sub, CONV_ROWS)
            tap = ubuf_ref[rows, :] if s == 0 else shift_ref[s - 1, rows, :]
            acc = acc + jnp.tile(wdw_ref[t], (CONV_ROWS // sub, 1)) * tap
        cen = acc - jnp.mean(acc, axis=-1, keepdims=True)
        var = jnp.mean(cen * cen, axis=-1, keepdims=True)
        z = cen * lax.rsqrt(var + EPS) * lng_ref[...] + lnb_ref[...]
        z = z * jax.nn.sigmoid(z)
        z = z * lax.rsqrt(jnp.mean(z * z, axis=-1, keepdims=True) + EPS) * gco_ref[...]
        ua_ref[0, r0:r0 + CONV_ROWS, :] = z.astype(BF16)
    ubuf_ref[0:CONV_HALO, :] = ubuf_ref[tm:tm + CONV_HALO, :]

    qkv = jnp.dot(h, win_ref[:, 2 * d_conv:], preferred_element_type=F32)
    qkv = jnp.concatenate([qkv[:, :d_attn] * q_scale, qkv[:, d_attn:]], axis=-1)
    qkv1_ref[0, 0] = qkv.astype(BF16)
    n_slab = qkv.shape[1] // lanes
    for i in range(n_slab):
        slab_ref[i] = qkv[:, i * lanes:(i + 1) * lanes]
    for rho4 in range(4):
        for i in range(n_slab):
            cols = slice(i * lanes, (i + 1) * lanes)
            t4 = slab_ref[i, pl.ds(rho4, tm // 4, stride=4), :]
            qkv4_ref[0, rho4, :, cols] = t4.astype(BF16)
            slab4_ref[i * 4 + rho4] = t4
    for rho4 in range(4):
        for rho16 in range(4):
            for i in range(n_slab):
                cols = slice(i * lanes, (i + 1) * lanes)
                t16 = slab4_ref[i * 4 + rho4, pl.ds(rho16, tm // 16, stride=4), :]
                qkv16_ref[0, rho4 + 4 * rho16, :, cols] = t16.astype(BF16)


def _mix_in(x, mod, g_norm, w_in, w_dw, b_dw, ln_g, ln_b, g_co, *, d_conv, d_attn):
    b, s, d = x.shape
    tm = ROW_TILE
    d_in = w_in.shape[1]
    d_qkv = 3 * d_attn
    vec = lambda a: a.reshape(1, -1)
    dils = [r for _, r in DILATED_PATTERNS]
    w_dw_rows = jnp.broadcast_to(w_dw[:, None, :], (CONV_K, V7X_SUBLANES, d_conv))
    assert dils == [1, 4, 16] and tm % (16 * 16) == 0
    qkv_spec = lambda r: pl.BlockSpec((1, r, tm // r, d_qkv), lambda i, j: (i, 0, j, 0))
    qkv_shape = lambda r: jax.ShapeDtypeStruct((b, r, s // r, d_qkv), BF16)
    return pl.pallas_call(
        functools.partial(_mix_in_kernel, d_conv=d_conv, d_attn=d_attn, q_scale=_qk_scale(d_attn // N_HEADS)),
        grid=(b, s // tm),
        in_specs=[
            pl.BlockSpec((1, tm, d), lambda i, j: (i, j, 0)),
            pl.BlockSpec((1, N_MOD, d), lambda i, j: (i, 0, 0)),
            _const_spec((1, d)),
            _const_spec((d, d_in)),
            _const_spec((CONV_K, V7X_SUBLANES, d_conv)),
            _const_spec((1, d_conv)),
            _const_spec((1, d_conv)),
            _const_spec((1, d_conv)),
            _const_spec((1, d_conv)),
        ],
        out_specs=[pl.BlockSpec((1, tm, d_conv), lambda i, j: (i, j, 0))] + [qkv_spec(r) for r in dils],
        out_shape=[jax.ShapeDtypeStruct((b, s, d_conv), BF16)] + [qkv_shape(r) for r in dils],
        scratch_shapes=[pltpu.VMEM((tm + CONV_HALO, d_conv), F32),
                        pltpu.VMEM((V7X_SUBLANES - 1, tm + CONV_HALO - V7X_SUBLANES, d_conv), F32),
                        pltpu.VMEM((d_qkv // V7X_LANES, tm, V7X_LANES), F32),
                        pltpu.VMEM((4 * d_qkv // V7X_LANES, tm // 4, V7X_LANES), F32)],
        compiler_params=pltpu.CompilerParams(dimension_semantics=("arbitrary", "arbitrary"),
                                             vmem_limit_bytes=VMEM_LIMIT),
        name="mix_in",
    )(x, mod, vec(g_norm), w_in, w_dw_rows, vec(b_dw), vec(ln_g), vec(ln_b), vec(g_co))


def _attn_kernel(q1_ref, k1_ref, v1_ref, q4_ref, k4_ref, v4_ref, q16_ref, k16_ref, v16_ref, bias_ref,
                 o_ref, acc_ref, m_ref, l_ref, *, head_dim):
    blk = ATTN_BLOCK
    s_len = acc_ref.shape[0]
    lane = lax.broadcasted_iota(jnp.int32, (blk, V7X_LANES), 1)
    head0 = lane < head_dim
    nt = (((1,), (1,)), ((), ()))

    def block_stats(qr, kr, vr, rho, n, with_prev):
        rows = pl.ds(n * blk, blk)
        if with_prev:
            krows = pl.ds((n - 1) * blk, 2 * blk)
            bias = bias_ref[...]
        else:
            krows = rows
            bias = bias_ref[:, blk:]
        q2 = qr[0, rho, rows, :]
        kk = kr[0, rho, krows, :]
        vv = vr[0, rho, krows, :]
        zero = jnp.zeros_like(q2)
        qs = jnp.concatenate([jnp.where(head0, q2, zero), jnp.where(head0, zero, q2)], axis=0)
        s = lax.dot_general(qs, kk, nt, preferred_element_type=F32) + bias
        m = jnp.max(s, axis=-1, keepdims=True)
        p = jnp.exp2(s - m).astype(BF16)
        one = jnp.ones_like(vv)
        vlane = lax.broadcasted_iota(jnp.int32, vv.shape, 1) < head_dim
        out0 = jnp.dot(p[:blk], jnp.where(vlane, vv, one), preferred_element_type=F32)
        out1 = jnp.dot(p[blk:], jnp.where(vlane, one, vv), preferred_element_type=F32)
        o2 = jnp.where(head0, out0, out1)
        l2 = pltpu.roll(jnp.where(head0, out1, out0), head_dim, axis=1)
        m2 = jnp.where(head0, jnp.broadcast_to(m[:blk], (blk, V7X_LANES)), jnp.broadcast_to(m[blk:], (blk, V7X_LANES)))
        return o2, m2, l2

    def merge(idx, o2, m2, l2, final_rows=None):
        m_old = m_ref[idx, :]
        m_new = jnp.maximum(m_old, m2)
        a = jnp.exp2(m_old - m_new)
        c = jnp.exp2(m2 - m_new)
        acc = a * acc_ref[idx, :] + c * o2
        den = a * l_ref[idx, :] + c * l2
        if final_rows is None:
            acc_ref[idx, :] = acc
            l_ref[idx, :] = den
            m_ref[idx, :] = m_new
        else:
            o_ref[0, final_rows, :] = (acc / den).astype(BF16)

    r = 16
    assert s_len // r == blk
    for rho in range(r):
        o2, m2, l2 = block_stats(q16_ref, k16_ref, v16_ref, rho, 0, False)
        idx = pl.ds(rho, blk, stride=r)
        acc_ref[idx, :] = o2
        m_ref[idx, :] = m2
        l_ref[idx, :] = l2

    r = 4
    for n in range(s_len // r // blk):
        for rho in range(r):
            o2, m2, l2 = block_stats(q4_ref, k4_ref, v4_ref, rho, n, n > 0)
            merge(pl.ds(rho + n * blk * r, blk, stride=r), o2, m2, l2)

    for n in range(s_len // blk):
        o2, m2, l2 = block_stats(q1_ref, k1_ref, v1_ref, 0, n, n > 0)
        merge(pl.ds(n * blk, blk), o2, m2, l2, final_rows=pl.ds(n * blk, blk))


def _dilated_attn(qkv1, qkv4, qkv16, *, d_attn):
    b, _, s, _ = qkv1.shape
    blk = ATTN_BLOCK
    pair_w = V7X_LANES
    n_pair = d_attn // pair_w
    qi = jnp.arange(2 * blk)[:, None] % blk
    ki = jnp.arange(2 * blk)[None, :]
    bias = jnp.where(jnp.where(ki < blk, ki >= qi, ki - blk <= qi), 0.0, -jnp.inf).astype(F32)
    specs = []
    for arr in (qkv1, qkv4, qkv16):
        r, l_sub = arr.shape[1], arr.shape[2]
        for part in range(3):
            specs.append(pl.BlockSpec((1, r, l_sub, pair_w), lambda i, j, part=part: (i, 0, 0, part * n_pair + j)))
    return pl.pallas_call(
        functools.partial(_attn_kernel, head_dim=d_attn // N_HEADS),
        grid=(b, n_pair),
        in_specs=specs + [_const_spec(bias.shape)],
        out_specs=pl.BlockSpec((1, s, pair_w), lambda i, j: (i, 0, j)),
        out_shape=jax.ShapeDtypeStruct((b, s, d_attn), BF16),
        scratch_shapes=[pltpu.VMEM((s, pair_w), F32), pltpu.VMEM((s, pair_w), F32), pltpu.VMEM((s, pair_w), F32)],
        compiler_params=pltpu.CompilerParams(dimension_semantics=("arbitrary", "arbitrary"),
                                             vmem_limit_bytes=VMEM_LIMIT),
        name="dilated_attn",
    )(qkv1, qkv1, qkv1, qkv4, qkv4, qkv4, qkv16, qkv16, qkv16, bias)


def _mix_out_ffn_kernel(x_ref, mod_ref, ua_ref, att_ref, gat_ref, wout_ref, gffn_ref, wup_ref, wdw_ref, bdw_ref,
                        wdown_ref, gfin_ref, out_ref, act_ref, prev_ref, *, d_conv, d_ff):
    tm = x_ref.shape[1]
    sub = V7X_SUBLANES
    ck = FFN_CHUNK

    att = att_ref[0].astype(F32)
    att = att * lax.rsqrt(jnp.mean(att * att, axis=-1, keepdims=True) + EPS) * gat_ref[...]
    mixed = (jnp.dot(ua_ref[0], wout_ref[0:d_conv, :], preferred_element_type=F32)
             + jnp.dot(att.astype(BF16), wout_ref[d_conv:, :], preferred_element_type=F32))
    x1 = x_ref[0] + mod_ref[0, 2:3, :] * mixed

    y = x1 * lax.rsqrt(jnp.mean(x1 * x1, axis=-1, keepdims=True) + EPS) * gffn_ref[...]
    h = (y * (1.0 + mod_ref[0, 4:5, :]) + mod_ref[0, 3:4, :]).astype(BF16)

    @pl.when(pl.program_id(1) == 0)
    def _():
        prev_ref[...] = jnp.zeros(prev_ref.shape, F32)

    def conv_cols(c0):
        cols = slice(c0, c0 + ck)
        up = jnp.dot(h, wup_ref[:, cols], preferred_element_type=F32)
        ext = jnp.concatenate([prev_ref[:, cols], up], axis=0)
        prev_ref[:, cols] = up[tm - sub:, :]
        z = wdw_ref[0:1, cols] * ext
        for t in range(1, FFN_CONV_K):
            z = pltpu.roll(z, 1, axis=0) + wdw_ref[t:t + 1, cols] * ext
        return z[sub:, :] + bdw_ref[:, cols]

    for j in range(d_ff // ck):
        gate = conv_cols(j * ck)
        val = conv_cols(d_ff + j * ck)
        act_ref[:, j * ck:(j + 1) * ck] = (gate * jax.nn.sigmoid(gate) * val).astype(BF16)

    x2 = x1 + mod_ref[0, 5:6, :] * jnp.dot(act_ref[...], wdown_ref[...], preferred_element_type=F32)
    out_ref[0] = x2 * lax.rsqrt(jnp.mean(x2 * x2, axis=-1, keepdims=True) + EPS) * gfin_ref[...]


def _mix_out_ffn(x, mod, ua, att, g_attn, w_out, g_ffn, w_up, w_dw, b_dw, w_down, g_fin, *, d_conv):
    b, s, d = x.shape
    tm = ROW_TILE
    d_attn = att.shape[2]
    d_ff = w_down.shape[0]
    assert d_ff % FFN_CHUNK == 0
    row = lambda c: pl.BlockSpec((1, tm, c), lambda i, j: (i, j, 0))
    vec = lambda a: a.reshape(1, -1)
    return pl.pallas_call(
        functools.partial(_mix_out_ffn_kernel, d_conv=d_conv, d_ff=d_ff),
        grid=(b, s // tm),
        in_specs=[
            row(d),
            pl.BlockSpec((1, N_MOD, d), lambda i, j: (i, 0, 0)),
            row(d_conv), row(d_attn),
            _const_spec((1, d_attn)),
            _const_spec(w_out.shape),
            _const_spec((1, d)),
            _const_spec(w_up.shape),
            _const_spec(w_dw.shape),
            _const_spec((1, 2 * d_ff)),
            _const_spec(w_down.shape),
            _const_spec((1, d)),
        ],
        out_specs=row(d),
        out_shape=jax.ShapeDtypeStruct((b, s, d), F32),
        scratch_shapes=[pltpu.VMEM((tm, d_ff), BF16), pltpu.VMEM((V7X_SUBLANES, 2 * d_ff), F32)],
        compiler_params=pltpu.CompilerParams(dimension_semantics=("arbitrary", "arbitrary"),
                                             vmem_limit_bytes=VMEM_LIMIT),
        name="mix_out_ffn",
    )(x, mod, ua, att, vec(g_attn), w_out, vec(g_ffn), w_up, w_dw, vec(b_dw), w_down, vec(g_fin))


def kernel(x, c, w_ada, b_ada, g_norm_mix, w_in, w_conv_dw, b_conv_dw, ln_conv_g, ln_conv_b, g_conv_out,
           g_attn_out, w_out, g_norm_ffn, w_up, w_ffn_dw, b_ffn_dw, w_down, g_final):
    b, s, d = x.shape
    assert w_ada.shape[0] == 1, "the final RMSNorm is fused into the layer kernel: one layer only"
    d_conv = w_conv_dw.shape[2]
    d_attn = g_attn_out.shape[1]
    mod = _ada_mod(c, w_ada[0], b_ada[0]).reshape(b, N_MOD, d)
    ua, qkv1, qkv4, qkv16 = _mix_in(x, mod, g_norm_mix[0], w_in[0].astype(BF16), w_conv_dw[0], b_conv_dw[0],
                                    ln_conv_g[0], ln_conv_b[0], g_conv_out[0], d_conv=d_conv, d_attn=d_attn)
    att = _dilated_attn(qkv1, qkv4, qkv16, d_attn=d_attn)
    return _mix_out_ffn(x, mod, ua, att, g_attn_out[0], w_out[0].astype(BF16), g_norm_ffn[0],
                        w_up[0].astype(BF16), w_ffn_dw[0], b_ffn_dw[0], w_down[0].astype(BF16), g_final,
                        d_conv=d_conv)
```
